```python
import jax, jax.numpy as jnp
from jax import lax
import numpy as np

D_MODEL = 2048
BATCH = 4
SEQ = 2048
DEPTH = 1

CONV_CH = 1024
CONV_K = 31
N_HEADS = 8
QK_NOPE = 128
QK_ROPE = 64
V_HEAD = 128
QK_HEAD = QK_NOPE + QK_ROPE
Q_LORA = 768
KV_LORA = 512
ATTN_CH = N_HEADS * V_HEAD
MIX_WIDTH = CONV_CH + ATTN_CH
IN_COLS = 2 * CONV_CH + Q_LORA + KV_LORA + QK_ROPE
ROPE_THETA = 10000.0
Q_BLOCK = 128
D_FF = ((8 * D_MODEL // 3 + 255) // 256) * 256
EPS = 1e-6

kernel_name = "hymba_conformer_mla_sandwich_layer"


def rmsnorm(x, g):
    xf = x.astype(jnp.float32)
    y = xf * lax.rsqrt(jnp.mean(xf * xf, axis=-1, keepdims=True) + EPS)
    return (y * g.astype(jnp.float32)).astype(x.dtype)


def layernorm(x, g, b):
    xf = x.astype(jnp.float32)
    mu = jnp.mean(xf, axis=-1, keepdims=True)
    var = jnp.mean(jnp.square(xf - mu), axis=-1, keepdims=True)
    y = (xf - mu) * lax.rsqrt(var + EPS)
    return (y * g.astype(jnp.float32) + b.astype(jnp.float32)).astype(x.dtype)


def rope_tables(positions, dtype):
    inv_freq = ROPE_THETA ** (-jnp.arange(0, QK_ROPE, 2, dtype=jnp.float32) / QK_ROPE)
    ang = positions.astype(jnp.float32)[..., None] * inv_freq
    return jnp.cos(ang).astype(dtype), jnp.sin(ang).astype(dtype)


def apply_rope(x, cos, sin):
    x1, x2 = jnp.split(x, 2, axis=-1)
    return jnp.concatenate([x1 * cos - x2 * sin, x2 * cos + x1 * sin], axis=-1)


def causal_depthwise_conv(u, w, b):
    y = lax.conv_general_dilated(
        u, w[:, None, :], window_strides=(1,), padding=[(CONV_K - 1, 0)],
        dimension_numbers=("NWC", "WIO", "NWC"), feature_group_count=u.shape[-1])
    return y + b


def causal_attention(q, k, v):
    B, S, H, Dq = q.shape
    nblk = S // Q_BLOCK
    qb = q.reshape(B, nblk, Q_BLOCK, H, Dq).transpose(1, 0, 2, 3, 4)
    kpos = jnp.arange(S)
    scale = Dq ** -0.5
    neg = jnp.finfo(jnp.float32).min

    def one_block(args):
        i, qi = args
        s = jnp.einsum('bqhd,bkhd->bhqk', qi, k).astype(jnp.float32) * scale
        qpos = i * Q_BLOCK + jnp.arange(Q_BLOCK)
        s = jnp.where(kpos[None, :] <= qpos[:, None], s, neg)
        p = jax.nn.softmax(s, axis=-1).astype(v.dtype)
        return jnp.einsum('bhqk,bkhd->bqhd', p, v)

    out = lax.map(one_block, (jnp.arange(nblk), qb))
    return out.transpose(1, 0, 2, 3, 4).reshape(B, S, H * v.shape[-1])


def setup_inputs(seed: int = 0) -> dict:
    key = jax.random.key(seed)
    ks = jax.random.split(key, 24)
    f = jnp.float32
    L = DEPTH

    def w(k, shape, fan_in):
        return jax.random.normal(k, shape, f) * (fan_in ** -0.5)

    def gain(k, n):
        return jnp.ones((L, n), f) + 0.05 * jax.random.normal(k, (L, n), f)

    x = jax.random.normal(ks[0], (BATCH, SEQ, D_MODEL), f)
    offset = jax.random.randint(ks[1], (BATCH, 1), 0, 1024, dtype=jnp.int32)
    positions = offset + jnp.arange(SEQ, dtype=jnp.int32)[None, :]
    return {
        "x": x,
        "positions": positions,
        "pre_mix_norm": gain(ks[2], D_MODEL),
        "w_in": w(ks[3], (L, D_MODEL, IN_COLS), D_MODEL),
        "q_norm": gain(ks[4], Q_LORA),
        "w_uq": w(ks[5], (L, Q_LORA, N_HEADS * QK_HEAD), Q_LORA),
        "kv_norm": gain(ks[6], KV_LORA),
        "w_ukv": w(ks[7], (L, KV_LORA, N_HEADS * (QK_NOPE + V_HEAD)), KV_LORA),
        "conv_w": w(ks[8], (L, CONV_K, CONV_CH), CONV_K),
        "conv_b": 0.02 * jax.random.normal(ks[9], (L, CONV_CH), f),
        "conv_ln_g": gain(ks[10], CONV_CH),
        "conv_ln_b": 0.02 * jax.random.normal(ks[11], (L, CONV_CH), f),
        "conv_out_norm": gain(ks[12], CONV_CH),
        "attn_out_norm": gain(ks[13], ATTN_CH),
        "w_out": w(ks[14], (L, MIX_WIDTH, D_MODEL), MIX_WIDTH),
        "post_mix_norm": gain(ks[15], D_MODEL),
        "pre_ffn_norm": gain(ks[16], D_MODEL),
        "w_gate": w(ks[17], (L, D_MODEL, D_FF), D_MODEL),
        "w_up": w(ks[18], (L, D_MODEL, D_FF), D_MODEL),
        "w_down": w(ks[19], (L, D_FF, D_MODEL), D_FF),
        "post_ffn_norm": gain(ks[20], D_MODEL),
    }


def reference(x, positions, pre_mix_norm, w_in, q_norm, w_uq, kv_norm, w_ukv,
              conv_w, conv_b, conv_ln_g, conv_ln_b, conv_out_norm, attn_out_norm,
              w_out, post_mix_norm, pre_ffn_norm, w_gate, w_up, w_down, post_ffn_norm):
    B, S, _ = x.shape
    cos, sin = rope_tables(positions, x.dtype)
    c1 = 2 * CONV_CH
    c2 = c1 + Q_LORA
    c3 = c2 + KV_LORA
    for l in range(DEPTH):
        h = rmsnorm(x, pre_mix_norm[l])
        z = h @ w_in[l]
        conv_in, q_lat, kv_lat, k_rope = z[..., :c1], z[..., c1:c2], z[..., c2:c3], z[..., c3:]

        a, g = jnp.split(conv_in, 2, axis=-1)
        u = a * jax.nn.sigmoid(g)
        u = causal_depthwise_conv(u, conv_w[l], conv_b[l])
        u = jax.nn.silu(layernorm(u, conv_ln_g[l], conv_ln_b[l]))

        q = (rmsnorm(q_lat, q_norm[l]) @ w_uq[l]).reshape(B, S, N_HEADS, QK_HEAD)
        q_nope, q_pe = q[..., :QK_NOPE], q[..., QK_NOPE:]
        q_pe = apply_rope(q_pe, cos[:, :, None, :], sin[:, :, None, :])
        kv = (rmsnorm(kv_lat, kv_norm[l]) @ w_ukv[l]).reshape(B, S, N_HEADS, QK_NOPE + V_HEAD)
        k_nope, v = kv[..., :QK_NOPE], kv[..., QK_NOPE:]
        k_pe = apply_rope(k_rope, cos, sin)
        k_pe = jnp.broadcast_to(k_pe[:, :, None, :], (B, S, N_HEADS, QK_ROPE))
        q_full = jnp.concatenate([q_nope, q_pe], axis=-1)
        k_full = jnp.concatenate([k_nope, k_pe], axis=-1)
        attn = causal_attention(q_full, k_full, v)

        mix = jnp.concatenate([rmsnorm(u, conv_out_norm[l]),
                               rmsnorm(attn, attn_out_norm[l])], axis=-1) @ w_out[l]
        x = x + rmsnorm(mix, post_mix_norm[l])

        hf = rmsnorm(x, pre_ffn_norm[l])
        ff = (jax.nn.silu(hf @ w_gate[l]) * (hf @ w_up[l])) @ w_down[l]
        x = x + rmsnorm(ff, post_ffn_norm[l])
    return x
```

```python
import functools

import jax
import jax.numpy as jnp
from jax import lax
from jax.experimental import pallas as pl
from jax.experimental.pallas import tpu as pltpu

D_MODEL = 2048
CONV_CH = 1024
CONV_K = 31
N_HEADS = 8
QK_NOPE = 128
QK_ROPE = 64
V_HEAD = 128
QK_HEAD = QK_NOPE + QK_ROPE
Q_LORA = 768
KV_LORA = 512
ATTN_CH = N_HEADS * V_HEAD
D_FF = 5632
ROPE_THETA = 10000.0
EPS = 1e-6

LANES = 128
HALF_ROPE = QK_ROPE // 2
HEAD_PAD = QK_NOPE + LANES
N_SLABS = CONV_CH // LANES
CONV_HALO = 32
V7X_VMEM_BYTES = 64 * 1024 * 1024
NEG_BIG = -1e30

C_Q = 2 * CONV_CH
C_KV = C_Q + Q_LORA
C_KR = C_KV + KV_LORA
IN_COLS_PAD = C_KR + LANES

TM_PROJ = 512
TS_CONV = 512
TQ_ATTN = 512
TM_FFN = 512
TF_FFN = 512

BF16 = jnp.bfloat16
F32 = jnp.float32


def _vmem_limit(*buffer_bytes):
    need = 2 * sum(buffer_bytes)
    return int(min(max(need, 16 * 1024 * 1024), V7X_VMEM_BYTES - 4 * 1024 * 1024))


def _nbytes(shape, dtype):
    n = 1
    for s in shape:
        n *= s
    return n * jnp.dtype(dtype).itemsize


def _dot(a, b):
    return jnp.dot(a, b, preferred_element_type=F32)


def _rms(x, g):
    return x * lax.rsqrt(jnp.mean(x * x, axis=-1, keepdims=True) + EPS) * g


def _sigmoid(x):
    return 1.0 / (1.0 + jnp.exp(-x))


def _resident(shape):
    return pl.BlockSpec(shape, lambda *_: (0,) * len(shape), pipeline_mode=pl.Buffered(1))


def _in_proj_kernel(x_ref, pos_ref, freq_ref, g_pre_ref, w_in_ref, g_q_ref, w_uq_ref,
                    g_kv_ref, w_ukv_ref, u0_ref, q_ref, k_ref, v_ref):
    h = _rms(x_ref[...], g_pre_ref[...]).astype(BF16)

    a = _dot(h, w_in_ref[:, 0:CONV_CH])
    g = _dot(h, w_in_ref[:, CONV_CH:C_Q])
    u0 = a * _sigmoid(g)
    for c in range(N_SLABS):
        u0_ref[c] = u0[:, c * LANES:(c + 1) * LANES]

    ang = pos_ref[...].astype(F32) * freq_ref[...]
    lane = lax.broadcasted_iota(jnp.int32, ang.shape, 1)
    first_half = lane < QK_ROPE
    table = jnp.where(first_half, jnp.cos(ang), jnp.sin(ang))

    def rope(block):
        r = block * table
        return r + pltpu.roll(r, QK_ROPE, axis=1)

    scale = QK_HEAD ** -0.5
    qn = _rms(_dot(h, w_in_ref[:, C_Q:C_KV]), g_q_ref[...]).astype(BF16)
    q_all = _dot(qn, w_uq_ref[...]) * scale
    for hh in range(N_HEADS):
        o = hh * HEAD_PAD
        q_ref[hh, :, 0:QK_NOPE] = q_all[:, o:o + QK_NOPE].astype(BF16)
        q_ref[hh, :, QK_NOPE:HEAD_PAD] = rope(q_all[:, o + QK_NOPE:o + HEAD_PAD]).astype(BF16)

    kvn = _rms(_dot(h, w_in_ref[:, C_KV:C_KR]), g_kv_ref[...]).astype(BF16)
    kv_all = _dot(kvn, w_ukv_ref[...])
    k_pe = jnp.where(first_half, rope(_dot(h, w_in_ref[:, C_KR:IN_COLS_PAD])), 0.0).astype(BF16)
    for hh in range(N_HEADS):
        o = hh * (QK_NOPE + V_HEAD)
        k_ref[hh, :, 0:QK_NOPE] = kv_all[:, o:o + QK_NOPE].astype(BF16)
        k_ref[hh, :, QK_NOPE:HEAD_PAD] = k_pe
        v_ref[hh] = kv_all[:, o + QK_NOPE:o + QK_NOPE + V_HEAD].astype(BF16)


def _in_proj(x2, pos2, freq, g_pre, w_in_p, g_q, w_uq_p, g_kv, w_ukv_b):
    t = x2.shape[0]
    tm = TM_PROJ
    row = lambda i: (i, 0)
    head_row = lambda i: (0, i, 0)
    out_shapes = (
        jax.ShapeDtypeStruct((N_SLABS, t, LANES), F32),
        jax.ShapeDtypeStruct((N_HEADS, t, HEAD_PAD), BF16),
        jax.ShapeDtypeStruct((N_HEADS, t, HEAD_PAD), BF16),
        jax.ShapeDtypeStruct((N_HEADS, t, V_HEAD), BF16),
    )
    limit = _vmem_limit(
        2 * _nbytes((tm, D_MODEL), F32), _nbytes(w_in_p.shape, BF16), _nbytes(w_uq_p.shape, BF16),
        _nbytes(w_ukv_b.shape, BF16), 2 * _nbytes((N_SLABS, tm, LANES), F32),
        4 * _nbytes((N_HEADS, tm, HEAD_PAD), BF16), 2 * _nbytes((N_HEADS, tm, V_HEAD), BF16))
    return pl.pallas_call(
        _in_proj_kernel,
        grid=(t // tm,),
        in_specs=[
            pl.BlockSpec((tm, D_MODEL), row),
            pl.BlockSpec((tm, 1), row),
            _resident((1, LANES)),
            _resident((1, D_MODEL)),
            _resident(w_in_p.shape),
            _resident((1, Q_LORA)),
            _resident(w_uq_p.shape),
            _resident((1, KV_LORA)),
            _resident(w_ukv_b.shape),
        ],
        out_specs=(
            pl.BlockSpec((N_SLABS, tm, LANES), head_row),
            pl.BlockSpec((N_HEADS, tm, HEAD_PAD), head_row),
            pl.BlockSpec((N_HEADS, tm, HEAD_PAD), head_row),
            pl.BlockSpec((N_HEADS, tm, V_HEAD), head_row),
        ),
        out_shape=out_shapes,
        compiler_params=pltpu.CompilerParams(
            dimension_semantics=("arbitrary",), vmem_limit_bytes=limit),
        name="in_proj",
    )(x2, pos2, freq, g_pre, w_in_p, g_q, w_uq_p, g_kv, w_ukv_b)


def _conv_kernel(u_ref, halo_ref, w_ref, cb_ref, lng_ref, lnb_ref, go_ref, o_ref, win, y_scr):
    ts = u_ref.shape[1]
    has_history = pl.program_id(1) > 0
    for c in range(N_SLABS):
        win[c, 0:CONV_HALO, :] = jnp.where(has_history, halo_ref[c], 0.0)
        win[c, CONV_HALO:CONV_HALO + ts, :] = u_ref[c]

    rows = 128
    first_tap = CONV_HALO - (CONV_K - 1)
    for c in range(N_SLABS):
        def conv_chunk(rc, carry, c=c):
            r0 = pl.multiple_of(rc * rows, rows)
            acc = jnp.broadcast_to(cb_ref[c], (rows, LANES))
            for k in range(CONV_K):
                acc = acc + w_ref[c, k:k + 1, :] * win[c, pl.ds(r0 + first_tap + k, rows), :]
            y_scr[c, pl.ds(r0, rows), :] = acc
            return carry
        lax.fori_loop(0, ts // rows, conv_chunk, 0)

    rows_n = 32
    inv_ch = 1.0 / CONV_CH

    def norm_chunk(rc, carry):
        r0 = pl.multiple_of(rc * rows_n, rows_n)
        ys = [y_scr[c, pl.ds(r0, rows_n), :] for c in range(N_SLABS)]
        mu = jnp.sum(functools.reduce(lambda p, q: p + q, ys), axis=-1, keepdims=True) * inv_ch
        ds = [y - mu for y in ys]
        var = jnp.sum(functools.reduce(lambda p, q: p + q, [d * d for d in ds]),
                      axis=-1, keepdims=True) * inv_ch
        rs = lax.rsqrt(var + EPS)
        zs = []
        for c in range(N_SLABS):
            z = ds[c] * rs * lng_ref[c] + lnb_ref[c]
            zs.append(z * _sigmoid(z))
        ms = jnp.sum(functools.reduce(lambda p, q: p + q, [z * z for z in zs]),
                     axis=-1, keepdims=True) * inv_ch
        rs2 = lax.rsqrt(ms + EPS)
        for c in range(N_SLABS):
            o_ref[pl.ds(r0, rows_n), c * LANES:(c + 1) * LANES] = (zs[c] * rs2 * go_ref[c]).astype(BF16)
        return carry

    lax.fori_loop(0, ts // rows_n, norm_chunk, 0)


def _conv(u0, w_slab, cb, lng, lnb, go, batch, seq):
    ts = TS_CONV
    ns = seq // ts
    halo_per_tile = ts // CONV_HALO
    slab_vec = (N_SLABS, 1, LANES)
    limit = _vmem_limit(
        2 * _nbytes((N_SLABS, ts, LANES), F32), 2 * _nbytes((N_SLABS, CONV_HALO, LANES), F32),
        _nbytes((N_SLABS, ts + CONV_HALO, LANES), F32), _nbytes((N_SLABS, ts, LANES), F32),
        2 * _nbytes((ts, CONV_CH), BF16))
    return pl.pallas_call(
        _conv_kernel,
        grid=(batch, ns),
        in_specs=[
            pl.BlockSpec((N_SLABS, ts, LANES), lambda b, j: (0, b * ns + j, 0)),
            pl.BlockSpec((N_SLABS, CONV_HALO, LANES),
                         lambda b, j: (0, jnp.maximum((b * ns + j) * halo_per_tile - 1, 0), 0)),
            _resident(w_slab.shape),
            _resident(slab_vec), _resident(slab_vec), _resident(slab_vec), _resident(slab_vec),
        ],
        out_specs=pl.BlockSpec((ts, CONV_CH), lambda b, j: (b * ns + j, 0)),
        out_shape=jax.ShapeDtypeStruct((batch * seq, CONV_CH), BF16),
        scratch_shapes=[
            pltpu.VMEM((N_SLABS, ts + CONV_HALO, LANES), F32),
            pltpu.VMEM((N_SLABS, ts, LANES), F32),
        ],
        compiler_params=pltpu.CompilerParams(
            dimension_semantics=("arbitrary", "arbitrary"), vmem_limit_bytes=limit),
        name="conv",
    )(u0, u0, w_slab, cb, lng, lnb, go)


def _attn_kernel(q_ref, k_ref, v_ref, g_ref, o_ref, o_scr):
    tq = q_ref.shape[1]
    qi = pl.program_id(1)
    row = lax.broadcasted_iota(jnp.int32, (tq, tq), 0)
    col = lax.broadcasted_iota(jnp.int32, (tq, tq), 1)
    causal = col <= row

    def scores(q, hh, kb):
        k = k_ref[hh, pl.ds(pl.multiple_of(kb * tq, tq), tq), :]
        return lax.dot_general(q, k, (((1,), (1,)), ((), ())), preferred_element_type=F32)

    def update(s, hh, kb, carry):
        m, l, acc = carry
        v = v_ref[hh, pl.ds(pl.multiple_of(kb * tq, tq), tq), :]
        m_new = jnp.maximum(m, jnp.max(s, axis=-1, keepdims=True))
        alpha = jnp.exp(m - m_new)
        p = jnp.exp(s - m_new)
        l = alpha * l + jnp.sum(p, axis=-1, keepdims=True)
        acc = alpha * acc + _dot(p.astype(BF16), v)
        return m_new, l, acc

    def head(hh, carry):
        q = q_ref[hh]
        init = (jnp.full((tq, 1), NEG_BIG, F32), jnp.zeros((tq, 1), F32),
                jnp.zeros((tq, V_HEAD), F32))
        state = lax.fori_loop(0, qi, lambda kb, st: update(scores(q, hh, kb), hh, kb, st), init)
        s_diag = jnp.where(causal, scores(q, hh, qi), NEG_BIG)
        _, l, acc = update(s_diag, hh, qi, state)
        o_scr[hh] = acc / l
        return carry

    lax.fori_loop(0, N_HEADS, head, 0)

    outs = [o_scr[hh] for hh in range(N_HEADS)]
    ms = jnp.sum(functools.reduce(lambda p, q: p + q, [o * o for o in outs]),
                 axis=-1, keepdims=True) * (1.0 / ATTN_CH)
    rs = lax.rsqrt(ms + EPS)
    for hh in range(N_HEADS):
        sl = slice(hh * V_HEAD, (hh + 1) * V_HEAD)
        o_ref[:, sl] = (outs[hh] * rs * g_ref[:, sl]).astype(BF16)


def _attention(q, k, v, g, batch, seq):
    tq = TQ_ATTN
    nq = seq // tq
    limit = _vmem_limit(
        2 * _nbytes((N_HEADS, tq, HEAD_PAD), BF16), 2 * _nbytes((N_HEADS, seq, HEAD_PAD), BF16),
        2 * _nbytes((N_HEADS, seq, V_HEAD), BF16), 2 * _nbytes((tq, ATTN_CH), BF16),
        _nbytes((N_HEADS, tq, V_HEAD), F32), 4 * _nbytes((tq, tq), F32))
    return pl.pallas_call(
        _attn_kernel,
        grid=(batch, nq),
        in_specs=[
            pl.BlockSpec((N_HEADS, tq, HEAD_PAD), lambda b, i: (0, b * nq + i, 0)),
            pl.BlockSpec((N_HEADS, seq, HEAD_PAD), lambda b, i: (0, b, 0)),
            pl.BlockSpec((N_HEADS, seq, V_HEAD), lambda b, i: (0, b, 0)),
            _resident((1, ATTN_CH)),
        ],
        out_specs=pl.BlockSpec((tq, ATTN_CH), lambda b, i: (b * nq + i, 0)),
        out_shape=jax.ShapeDtypeStruct((batch * seq, ATTN_CH), BF16),
        scratch_shapes=[pltpu.VMEM((N_HEADS, tq, V_HEAD), F32)],
        compiler_params=pltpu.CompilerParams(
            dimension_semantics=("arbitrary", "arbitrary"), vmem_limit_bytes=limit),
        name="attention",
    )(q, k, v, g)


def _out_proj_kernel(uc_ref, at_ref, x_ref, w_ref, g_pm_ref, g_pf_ref, x1_ref, hf_ref):
    mix = _dot(uc_ref[...], w_ref[0:CONV_CH, :]) + _dot(at_ref[...], w_ref[CONV_CH:, :])
    x1 = x_ref[...] + _rms(mix, g_pm_ref[...])
    x1_ref[...] = x1
    hf_ref[...] = _rms(x1, g_pf_ref[...]).astype(BF16)


def _out_proj(uc, at, x2, w_out_b, g_pm, g_pf):
    t = x2.shape[0]
    tm = TM_PROJ
    row = lambda i: (i, 0)
    limit = _vmem_limit(
        4 * _nbytes((tm, CONV_CH), BF16), 4 * _nbytes((tm, D_MODEL), F32),
        _nbytes(w_out_b.shape, BF16), 2 * _nbytes((tm, D_MODEL), BF16))
    return pl.pallas_call(
        _out_proj_kernel,
        grid=(t // tm,),
        in_specs=[
            pl.BlockSpec((tm, CONV_CH), row),
            pl.BlockSpec((tm, ATTN_CH), row),
            pl.BlockSpec((tm, D_MODEL), row),
            _resident(w_out_b.shape),
            _resident((1, D_MODEL)),
            _resident((1, D_MODEL)),
        ],
        out_specs=(pl.BlockSpec((tm, D_MODEL), row), pl.BlockSpec((tm, D_MODEL), row)),
        out_shape=(jax.ShapeDtypeStruct((t, D_MODEL), F32), jax.ShapeDtypeStruct((t, D_MODEL), BF16)),
        compiler_params=pltpu.CompilerParams(
            dimension_semantics=("arbitrary",), vmem_limit_bytes=limit),
        name="out_proj",
    )(uc, at, x2, w_out_b, g_pm, g_pf)


def _ffn_kernel(hf_ref, x1_ref, wg_ref, wu_ref, wd_ref, g_ref, o_ref):
    j = pl.program_id(1)

    @pl.when(j == 0)
    def _():
        o_ref[...] = jnp.zeros_like(o_ref)

    hf = hf_ref[...]
    gate = _dot(hf, wg_ref[...])
    up = _dot(hf, wu_ref[...])
    act = (gate * _sigmoid(gate) * up).astype(BF16)
    o_ref[...] += _dot(act, wd_ref[...])

    @pl.when(j == pl.num_programs(1) - 1)
    def _():
        o_ref[...] = x1_ref[...] + _rms(o_ref[...], g_ref[...])


def _ffn(hf, x1, wg_b, wu_b, wd_b, g):
    t = hf.shape[0]
    tm, tf = TM_FFN, TF_FFN
    limit = _vmem_limit(
        2 * _nbytes((tm, D_MODEL), BF16), 4 * _nbytes((tm, D_MODEL), F32),
        6 * _nbytes((D_MODEL, tf), BF16), 3 * _nbytes((tm, tf), F32))
    return pl.pallas_call(
        _ffn_kernel,
        grid=(t // tm, D_FF // tf),
        in_specs=[
            pl.BlockSpec((tm, D_MODEL), lambda i, j: (i, 0)),
            pl.BlockSpec((tm, D_MODEL), lambda i, j: (i, 0)),
            pl.BlockSpec((D_MODEL, tf), lambda i, j: (0, j)),
            pl.BlockSpec((D_MODEL, tf), lambda i, j: (0, j)),
            pl.BlockSpec((tf, D_MODEL), lambda i, j: (j, 0)),
            _resident((1, D_MODEL)),
        ],
        out_specs=pl.BlockSpec((tm, D_MODEL), lambda i, j: (i, 0)),
        out_shape=jax.ShapeDtypeStruct((t, D_MODEL), F32),
        compiler_params=pltpu.CompilerParams(
            dimension_semantics=("arbitrary", "arbitrary"), vmem_limit_bytes=limit),
        name="ffn",
    )(hf, x1, wg_b, wu_b, wd_b, g)


def _rope_columns(w):
    w1, w2 = w[:, :HALF_ROPE], w[:, HALF_ROPE:]
    return jnp.concatenate([w1, w2, -w2, w1], axis=1)


def _slab_vec(v):
    return v.reshape(N_SLABS, 1, LANES)


def kernel(x, positions, pre_mix_norm, w_in, q_norm, w_uq, kv_norm, w_ukv, conv_w, conv_b,
           conv_ln_g, conv_ln_b, conv_out_norm, attn_out_norm, w_out, post_mix_norm,
           pre_ffn_norm, w_gate, w_up, w_down, post_ffn_norm):
    batch, seq, _ = x.shape
    depth = w_in.shape[0]
    tokens = batch * seq
    x2 = x.reshape(tokens, D_MODEL)
    pos2 = positions.reshape(tokens, 1)
    inv_freq = ROPE_THETA ** (-jnp.arange(0, QK_ROPE, 2, dtype=F32) / QK_ROPE)
    freq = jnp.tile(inv_freq, LANES // HALF_ROPE).reshape(1, LANES)

    for l in range(depth):
        w_in_p = jnp.concatenate([w_in[l][:, :C_KR], _rope_columns(w_in[l][:, C_KR:])], axis=1).astype(BF16)
        wq = w_uq[l].reshape(Q_LORA, N_HEADS, QK_HEAD)
        wq_rope = jnp.concatenate(
            [wq[..., QK_NOPE:QK_NOPE + HALF_ROPE], wq[..., QK_NOPE + HALF_ROPE:],
             -wq[..., QK_NOPE + HALF_ROPE:], wq[..., QK_NOPE:QK_NOPE + HALF_ROPE]], axis=-1)
        w_uq_p = jnp.concatenate([wq[..., :QK_NOPE], wq_rope], axis=-1).reshape(
            Q_LORA, N_HEADS * HEAD_PAD).astype(BF16)
        w_ukv_b = w_ukv[l].astype(BF16)
        w_out_b = w_out[l].astype(BF16)
        wg_b, wu_b, wd_b = w_gate[l].astype(BF16), w_up[l].astype(BF16), w_down[l].astype(BF16)
        conv_w_pad = jnp.pad(conv_w[l], ((0, CONV_HALO - CONV_K), (0, 0)))
        w_slab = conv_w_pad.reshape(CONV_HALO, N_SLABS, LANES).transpose(1, 0, 2)

        u0, q, k, v = _in_proj(
            x2, pos2, freq, pre_mix_norm[l].reshape(1, -1), w_in_p, q_norm[l].reshape(1, -1),
            w_uq_p, kv_norm[l].reshape(1, -1), w_ukv_b)
        uc = _conv(u0, w_slab, _slab_vec(conv_b[l]), _slab_vec(conv_ln_g[l]),
                   _slab_vec(conv_ln_b[l]), _slab_vec(conv_out_norm[l]), batch, seq)
        at = _attention(q, k, v, attn_out_norm[l].reshape(1, -1), batch, seq)
        x1, hf = _out_proj(uc, at, x2, w_out_b, post_mix_norm[l].reshape(1, -1),
                           pre_ffn_norm[l].reshape(1, -1))
        x2 = _ffn(hf, x1, wg_b, wu_b, wd_b, post_ffn_norm[l].reshape(1, -1))
    return x2.reshape(batch, seq, D_MODEL)
```

```python
import functools

import jax
import jax.numpy as jnp
from jax import lax
from jax.experimental import pallas as pl
from jax.experimental.pallas import tpu as pltpu

D_MODEL = 2048
CONV_CH = 1024
CONV_K = 31
N_HEADS = 8
QK_NOPE = 128
QK_ROPE = 64
V_HEAD = 128
QK_HEAD = QK_NOPE + QK_ROPE
Q_LORA = 768
KV_LORA = 512
ATTN_CH = N_HEADS * V_HEAD
D_FF = 5632
ROPE_THETA = 10000.0
EPS = 1e-6

LANES = 128
HALF_ROPE = QK_ROPE // 2
HEAD_PAD = QK_NOPE + LANES
N_SLABS = CONV_CH // LANES
CONV_HALO = 32
V7X_VMEM_BYTES = 64 * 1024 * 1024
NEG_BIG = -1e30
LOG2_E = 1.4426950408889634

C_Q = 2 * CONV_CH
C_KV = C_Q + Q_LORA
C_KR = C_KV + KV_LORA

TM_PROJ = 512
TS_CONV = 512
TQ_ATTN = 512
TM_FFN = 512
TF_FFN = 512

BF16 = jnp.bfloat16
F32 = jnp.float32


def _vmem_limit(*buffer_bytes):
    need = 2 * sum(buffer_bytes)
    return int(min(max(need, 16 * 1024 * 1024), V7X_VMEM_BYTES - 4 * 1024 * 1024))


def _nbytes(shape, dtype):
    n = 1
    for s in shape:
        n *= s
    return n * jnp.dtype(dtype).itemsize


def _dot(a, b):
    return jnp.dot(a, b, preferred_element_type=F32)


def _rms(x, g):
    return x * lax.rsqrt(jnp.mean(x * x, axis=-1, keepdims=True) + EPS) * g


def _sigmoid(x):
    return 1.0 / (1.0 + jnp.exp(-x))


def _resident(shape):
    return pl.BlockSpec(shape, lambda *_: (0,) * len(shape), pipeline_mode=pl.Buffered(1))


def _in_proj_kernel(x_ref, pos_ref, freq_ref, g_pre_ref, w_in_ref, w_kr_ref, g_q_ref, w_uqt_ref,
                    g_kv_ref, w_uk_ref, w_uvt_ref, u0_ref, qt_ref, k_ref, vt_ref):
    h = _rms(x_ref[...], g_pre_ref[...]).astype(BF16)

    a = _dot(h, w_in_ref[:, 0:CONV_CH])
    g = _dot(h, w_in_ref[:, CONV_CH:C_Q])
    u0 = a * _sigmoid(g)
    for c in range(N_SLABS):
        u0_ref[c] = u0[:, c * LANES:(c + 1) * LANES]

    ang_t = freq_ref[...] * pos_ref[...].astype(F32)
    row = lax.broadcasted_iota(jnp.int32, ang_t.shape, 0)
    table_t = jnp.where(row < QK_ROPE, jnp.cos(ang_t), jnp.sin(ang_t))
    table = table_t.T

    scale = QK_HEAD ** -0.5 * LOG2_E
    qn =_rms(_dot(h, w_in_ref[:, C_Q:C_KV]), g_q_ref[...])
    q_all_t = _dot(w_uqt_ref[...], qn.T.astype(BF16)) * scale
    for hh in range(N_HEADS):
        o = hh * HEAD_PAD
        r = q_all_t[o + QK_NOPE:o + HEAD_PAD, :] * table_t
        qt_ref[hh, 0, 0:QK_NOPE, :] = q_all_t[o:o + QK_NOPE, :].astype(BF16)
        qt_ref[hh, 0, QK_NOPE:QK_HEAD, :] = (r[0:QK_ROPE, :] + r[QK_ROPE:, :]).astype(BF16)
        qt_ref[hh, 0, QK_HEAD:HEAD_PAD, :] = jnp.zeros((HEAD_PAD - QK_HEAD, r.shape[1]), BF16)

    kvn = _rms(_dot(h, w_in_ref[:, C_KV:C_KR]), g_kv_ref[...])
    k_all = _dot(kvn.astype(BF16), w_uk_ref[...])
    v_all_t = _dot(w_uvt_ref[...], kvn.T.astype(BF16))
    rk = _dot(h, w_kr_ref[...]) * table
    lane = lax.broadcasted_iota(jnp.int32, rk.shape, 1)
    k_pe = jnp.where(lane < QK_ROPE, rk + pltpu.roll(rk, QK_ROPE, axis=1), 0.0).astype(BF16)
    for hh in range(N_HEADS):
        k_ref[hh, :, 0:QK_NOPE] = k_all[:, hh * QK_NOPE:(hh + 1) * QK_NOPE].astype(BF16)
        k_ref[hh, :, QK_NOPE:HEAD_PAD] = k_pe
        vt_ref[hh, 0] = v_all_t[hh * V_HEAD:(hh + 1) * V_HEAD, :].astype(BF16)


def _in_proj(x2, pos_row, freq_col, g_pre, w_in_b, w_kr, g_q, w_uqt, g_kv, w_uk, w_uvt):
    t = x2.shape[0]
    tm = TM_PROJ
    nt = t // tm
    row = lambda i: (i, 0)
    out_shapes = (
        jax.ShapeDtypeStruct((N_SLABS, t, LANES), F32),
        jax.ShapeDtypeStruct((N_HEADS, nt, HEAD_PAD, tm), BF16),
        jax.ShapeDtypeStruct((N_HEADS, t, HEAD_PAD), BF16),
        jax.ShapeDtypeStruct((N_HEADS, nt, V_HEAD, tm), BF16),
    )
    limit = _vmem_limit(
        2 * _nbytes((tm, D_MODEL), F32), _nbytes(w_in_b.shape, BF16), _nbytes(w_kr.shape, BF16),
        _nbytes(w_uqt.shape, BF16), _nbytes(w_uk.shape, BF16), _nbytes(w_uvt.shape, BF16),
        2 * _nbytes((N_SLABS, tm, LANES), F32), 4 * _nbytes((N_HEADS, tm, HEAD_PAD), BF16),
        2 * _nbytes((N_HEADS, tm, V_HEAD), BF16))
    return pl.pallas_call(
        _in_proj_kernel,
        grid=(nt,),
        in_specs=[
            pl.BlockSpec((tm, D_MODEL), row),
            pl.BlockSpec((1, tm), lambda i: (0, i)),
            _resident((LANES, 1)),
            _resident((1, D_MODEL)),
            _resident(w_in_b.shape),
            _resident(w_kr.shape),
            _resident((1, Q_LORA)),
            _resident(w_uqt.shape),
            _resident((1, KV_LORA)),
            _resident(w_uk.shape),
            _resident(w_uvt.shape),
        ],
        out_specs=(
            pl.BlockSpec((N_SLABS, tm, LANES), lambda i: (0, i, 0)),
            pl.BlockSpec((N_HEADS, 1, HEAD_PAD, tm), lambda i: (0, i, 0, 0)),
            pl.BlockSpec((N_HEADS, tm, HEAD_PAD), lambda i: (0, i, 0)),
            pl.BlockSpec((N_HEADS, 1, V_HEAD, tm), lambda i: (0, i, 0, 0)),
        ),
        out_shape=out_shapes,
        compiler_params=pltpu.CompilerParams(
            dimension_semantics=("arbitrary",), vmem_limit_bytes=limit),
        name="in_proj",
    )(x2, pos_row, freq_col, g_pre, w_in_b, w_kr, g_q, w_uqt, g_kv, w_uk, w_uvt)


def _conv_kernel(u_ref, halo_ref, w_ref, cb_ref, lng_ref, lnb_ref, go_ref, o_ref, win, y_scr):
    ts = u_ref.shape[1]
    has_history = pl.program_id(1) > 0
    for c in range(N_SLABS):
        win[c, 0:CONV_HALO, :] = jnp.where(has_history, halo_ref[c], 0.0)
        win[c, CONV_HALO:CONV_HALO + ts, :] = u_ref[c]

    rows = 128
    first_tap = CONV_HALO - (CONV_K - 1)
    for c in range(N_SLABS):
        def conv_chunk(rc, carry, c=c):
            r0 = pl.multiple_of(rc * rows, rows)
            acc = jnp.broadcast_to(cb_ref[c], (rows, LANES))
            for k in range(CONV_K):
                acc = acc + w_ref[c, k:k + 1, :] * win[c, pl.ds(r0 + first_tap + k, rows), :]
            y_scr[c, pl.ds(r0, rows), :] = acc
            return carry
        lax.fori_loop(0, ts // rows, conv_chunk, 0)

    rows_n = 32
    inv_ch = 1.0 / CONV_CH

    def norm_chunk(rc, carry):
        r0 = pl.multiple_of(rc * rows_n, rows_n)
        ys = [y_scr[c, pl.ds(r0, rows_n), :] for c in range(N_SLABS)]
        mu = jnp.sum(functools.reduce(lambda p, q: p + q, ys), axis=-1, keepdims=True) * inv_ch
        ds = [y - mu for y in ys]
        var = jnp.sum(functools.reduce(lambda p, q: p + q, [d * d for d in ds]),
                      axis=-1, keepdims=True) * inv_ch
        rs = lax.rsqrt(var + EPS)
        zs = []
        for c in range(N_SLABS):
            z = ds[c] * rs * lng_ref[c] + lnb_ref[c]
            zs.append(z * _sigmoid(z))
        ms = jnp.sum(functools.reduce(lambda p, q: p + q, [z * z for z in zs]),
                     axis=-1, keepdims=True) * inv_ch
        rs2 = lax.rsqrt(ms + EPS)
        for c in range(N_SLABS):
            o_ref[pl.ds(r0, rows_n), c * LANES:(c + 1) * LANES] = (zs[c] * rs2 * go_ref[c]).astype(BF16)
        return carry

    lax.fori_loop(0, ts // rows_n, norm_chunk, 0)


def _conv(u0, w_slab, cb, lng, lnb, go, batch, seq):
    ts = TS_CONV
    ns = seq // ts
    halo_per_tile = ts // CONV_HALO
    slab_vec = (N_SLABS, 1, LANES)
    limit = _vmem_limit(
        2 * _nbytes((N_SLABS, ts, LANES), F32), 2 * _nbytes((N_SLABS, CONV_HALO, LANES), F32),
        _nbytes((N_SLABS, ts + CONV_HALO, LANES), F32), _nbytes((N_SLABS, ts, LANES), F32),
        2 * _nbytes((ts, CONV_CH), BF16))
    return pl.pallas_call(
        _conv_kernel,
        grid=(batch, ns),
        in_specs=[
            pl.BlockSpec((N_SLABS, ts, LANES), lambda b, j: (0, b * ns + j, 0)),
            pl.BlockSpec((N_SLABS, CONV_HALO, LANES),
                         lambda b, j: (0, jnp.maximum((b * ns + j) * halo_per_tile - 1, 0), 0)),
            _resident(w_slab.shape),
            _resident(slab_vec), _resident(slab_vec), _resident(slab_vec), _resident(slab_vec),
        ],
        out_specs=pl.BlockSpec((ts, CONV_CH), lambda b, j: (b * ns + j, 0)),
        out_shape=jax.ShapeDtypeStruct((batch * seq, CONV_CH), BF16),
        scratch_shapes=[
            pltpu.VMEM((N_SLABS, ts + CONV_HALO, LANES), F32),
            pltpu.VMEM((N_SLABS, ts, LANES), F32),
        ],
        compiler_params=pltpu.CompilerParams(
            dimension_semantics=("arbitrary", "arbitrary"), vmem_limit_bytes=limit),
        name="conv",
    )(u0, u0, w_slab, cb, lng, lnb, go)


def _attn_kernel(qt_ref, k_ref, vt_ref, g_ref, o_ref, m_scr, l_scr, acc_scr):
    tq = qt_ref.shape[3]
    qi = pl.program_id(1)
    key = lax.broadcasted_iota(jnp.int32, (tq, tq), 0)
    qry = lax.broadcasted_iota(jnp.int32, (tq, tq), 1)
    causal = key <= qry

    m_scr[...] = jnp.full(m_scr.shape, NEG_BIG, F32)
    l_scr[...] = jnp.zeros(l_scr.shape, F32)
    acc_scr[...] = jnp.zeros(acc_scr.shape, F32)

    def key_block(kb, diagonal):
        rows = pl.ds(pl.multiple_of(kb * tq, tq), tq)
        scores = lambda hh: _dot(k_ref[hh, rows, :], qt_ref[hh, 0])
        s_next = scores(0)
        for hh in range(N_HEADS):
            s = s_next
            if hh + 1 < N_HEADS:
                s_next = scores(hh + 1)
            if diagonal:
                s = jnp.where(causal, s, NEG_BIG)
            m = m_scr[hh]
            m_new = jnp.maximum(m, jnp.max(s, axis=0, keepdims=True))
            alpha = jnp.exp2(m - m_new)
            p = jnp.exp2(s - m_new)
            l_scr[hh] = alpha * l_scr[hh] + jnp.sum(p, axis=0, keepdims=True)
            acc_scr[hh] = alpha * acc_scr[hh] + _dot(vt_ref[hh, kb], p.astype(BF16))
            m_scr[hh] = m_new

    def full_block(kb, carry):
        key_block(kb, False)
        return carry

    lax.fori_loop(0, qi, full_block, 0)
    key_block(qi, True)

    ms = jnp.zeros((1, tq), F32)
    for hh in range(N_HEADS):
        o_t = acc_scr[hh] / l_scr[hh]
        acc_scr[hh] = o_t
        ms = ms + jnp.sum(o_t * o_t, axis=0, keepdims=True)
    rs = lax.rsqrt(ms * (1.0 / ATTN_CH) + EPS)
    for hh in range(N_HEADS):
        sl = slice(hh * V_HEAD, (hh + 1) * V_HEAD)
        o_ref[:, sl] = (acc_scr[hh] * rs * g_ref[sl, :]).T.astype(BF16)


def _attention(q_t, k, v_t, g, batch, seq):
    tq = TQ_ATTN
    nq = seq // tq
    limit = _vmem_limit(
        2 * _nbytes((N_HEADS, tq, HEAD_PAD), BF16), 2 * _nbytes((N_HEADS, seq, HEAD_PAD), BF16),
        2 * _nbytes((N_HEADS, seq, V_HEAD), BF16), 2 * _nbytes((tq, ATTN_CH), BF16),
        _nbytes((N_HEADS, tq, V_HEAD), F32), 4 * _nbytes((tq, tq), F32))
    return pl.pallas_call(
        _attn_kernel,
        grid=(batch, nq),
        in_specs=[
            pl.BlockSpec((N_HEADS, 1, HEAD_PAD, tq), lambda b, i: (0, b * nq + i, 0, 0)),
            pl.BlockSpec((N_HEADS, seq, HEAD_PAD), lambda b, i: (0, b, 0)),
            pl.BlockSpec((N_HEADS, nq, V_HEAD, tq), lambda b, i: (0, b, 0, 0)),
            _resident((ATTN_CH, 1)),
        ],
        out_specs=pl.BlockSpec((tq, ATTN_CH), lambda b, i: (b * nq + i, 0)),
        out_shape=jax.ShapeDtypeStruct((batch * seq, ATTN_CH), BF16),
        scratch_shapes=[
            pltpu.VMEM((N_HEADS, 1, tq), F32),
            pltpu.VMEM((N_HEADS, 1, tq), F32),
            pltpu.VMEM((N_HEADS, V_HEAD, tq), F32),
        ],
        compiler_params=pltpu.CompilerParams(
            dimension_semantics=("arbitrary", "arbitrary"), vmem_limit_bytes=limit),
        name="attention",
    )(q_t, k, v_t, g)


def _out_proj_kernel(uc_ref, at_ref, x_ref, w_ref, g_pm_ref, g_pf_ref, x1_ref, hf_ref):
    mix = _dot(uc_ref[...], w_ref[0:CONV_CH, :]) + _dot(at_ref[...], w_ref[CONV_CH:, :])
    x1 = x_ref[...] + _rms(mix, g_pm_ref[...])
    x1_ref[...] = x1
    hf_ref[...] = _rms(x1, g_pf_ref[...]).astype(BF16)


def _out_proj(uc, at, x2, w_out_b, g_pm, g_pf):
    t = x2.shape[0]
    tm = TM_PROJ
    row = lambda i: (i, 0)
    limit = _vmem_limit(
        4 * _nbytes((tm, CONV_CH), BF16), 4 * _nbytes((tm, D_MODEL), F32),
        _nbytes(w_out_b.shape, BF16), 2 * _nbytes((tm, D_MODEL), BF16))
    return pl.pallas_call(
        _out_proj_kernel,
        grid=(t // tm,),
        in_specs=[
            pl.BlockSpec((tm, CONV_CH), row),
            pl.BlockSpec((tm, ATTN_CH), row),
            pl.BlockSpec((tm, D_MODEL), row),
            _resident(w_out_b.shape),
            _resident((1, D_MODEL)),
            _resident((1, D_MODEL)),
        ],
        out_specs=(pl.BlockSpec((tm, D_MODEL), row), pl.BlockSpec((tm, D_MODEL), row)),
        out_shape=(jax.ShapeDtypeStruct((t, D_MODEL), F32), jax.ShapeDtypeStruct((t, D_MODEL), BF16)),
        compiler_params=pltpu.CompilerParams(
            dimension_semantics=("arbitrary",), vmem_limit_bytes=limit),
        name="out_proj",
    )(uc, at, x2, w_out_b, g_pm, g_pf)


def _ffn_kernel(hf_ref, x1_ref, wg_ref, wu_ref, wd_ref, g_ref, o_ref):
    j = pl.program_id(1)

    @pl.when(j == 0)
    def _():
        o_ref[...] = jnp.zeros_like(o_ref)

    hf = hf_ref[...]
    gate = _dot(hf, wg_ref[...])
    up = _dot(hf, wu_ref[...])
    act = (gate * _sigmoid(gate) * up).astype(BF16)
    o_ref[...] += _dot(act, wd_ref[...])

    @pl.when(j == pl.num_programs(1) - 1)
    def _():
        o_ref[...] = x1_ref[...] + _rms(o_ref[...], g_ref[...])


def _ffn(hf, x1, wg_b, wu_b, wd_b, g):
    t = hf.shape[0]
    tm, tf = TM_FFN, TF_FFN
    limit = _vmem_limit(
        2 * _nbytes((tm, D_MODEL), BF16), 4 * _nbytes((tm, D_MODEL), F32),
        6 * _nbytes((D_MODEL, tf), BF16), 3 * _nbytes((tm, tf), F32))
    return pl.pallas_call(
        _ffn_kernel,
        grid=(t // tm, D_FF // tf),
        in_specs=[
            pl.BlockSpec((tm, D_MODEL), lambda i, j: (i, 0)),
            pl.BlockSpec((tm, D_MODEL), lambda i, j: (i, 0)),
            pl.BlockSpec((D_MODEL, tf), lambda i, j: (0, j)),
            pl.BlockSpec((D_MODEL, tf), lambda i, j: (0, j)),
            pl.BlockSpec((tf, D_MODEL), lambda i, j: (j, 0)),
            _resident((1, D_MODEL)),
        ],
        out_specs=pl.BlockSpec((tm, D_MODEL), lambda i, j: (i, 0)),
        out_shape=jax.ShapeDtypeStruct((t, D_MODEL), F32),
        compiler_params=pltpu.CompilerParams(
            dimension_semantics=("arbitrary", "arbitrary"), vmem_limit_bytes=limit),
        name="ffn",
    )(hf, x1, wg_b, wu_b, wd_b, g)


def _rope_columns(w):
    w1, w2 = w[..., :HALF_ROPE], w[..., HALF_ROPE:]
    return jnp.concatenate([w1, w2, -w2, w1], axis=-1)


def _slab_vec(v):
    return v.reshape(N_SLABS, 1, LANES)


def kernel(x, positions, pre_mix_norm, w_in, q_norm, w_uq, kv_norm, w_ukv, conv_w, conv_b,
           conv_ln_g, conv_ln_b, conv_out_norm, attn_out_norm, w_out, post_mix_norm,
           pre_ffn_norm, w_gate, w_up, w_down, post_ffn_norm):
    batch, seq, _ = x.shape
    depth = w_in.shape[0]
    tokens = batch * seq
    x2 = x.reshape(tokens, D_MODEL)
    pos_row = positions.reshape(1, tokens)
    inv_freq = ROPE_THETA ** (-jnp.arange(0, QK_ROPE, 2, dtype=F32) / QK_ROPE)
    freq_col = jnp.tile(inv_freq, LANES // HALF_ROPE).reshape(LANES, 1)

    for l in range(depth):
        w_in_b = w_in[l].astype(BF16)
        w_kr = _rope_columns(w_in[l][:, C_KR:]).astype(BF16)
        wq = w_uq[l].reshape(Q_LORA, N_HEADS, QK_HEAD)
        w_uqt = jnp.concatenate([wq[..., :QK_NOPE], _rope_columns(wq[..., QK_NOPE:])], axis=-1).reshape(
            Q_LORA, N_HEADS * HEAD_PAD).T.astype(BF16)
        wkv = w_ukv[l].reshape(KV_LORA, N_HEADS, QK_NOPE + V_HEAD)
        w_uk = wkv[..., :QK_NOPE].reshape(KV_LORA, N_HEADS * QK_NOPE).astype(BF16)
        w_uvt = wkv[..., QK_NOPE:].reshape(KV_LORA, N_HEADS * V_HEAD).T.astype(BF16)
        w_out_b = w_out[l].astype(BF16)
        wg_b, wu_b, wd_b = w_gate[l].astype(BF16), w_up[l].astype(BF16), w_down[l].astype(BF16)
        conv_w_pad = jnp.pad(conv_w[l], ((0, CONV_HALO - CONV_K), (0, 0)))
        w_slab = conv_w_pad.reshape(CONV_HALO, N_SLABS, LANES).transpose(1, 0, 2)

        u0, q_t, k, v_t = _in_proj(
            x2, pos_row, freq_col, pre_mix_norm[l].reshape(1, -1), w_in_b, w_kr,
            q_norm[l].reshape(1, -1), w_uqt, kv_norm[l].reshape(1, -1), w_uk, w_uvt)
        uc = _conv(u0, w_slab, _slab_vec(conv_b[l]), _slab_vec(conv_ln_g[l]),
                   _slab_vec(conv_ln_b[l]), _slab_vec(conv_out_norm[l]), batch, seq)
        at = _attention(q_t, k, v_t, attn_out_norm[l].reshape(-1, 1), batch, seq)
        x1, hf = _out_proj(uc, at, x2, w_out_b, post_mix_norm[l].reshape(1, -1),
                           pre_ffn_norm[l].reshape(1, -1))
        x2 = _ffn(hf, x1, wg_b, wu_b, wd_b, post_ffn_norm[l].reshape(1, -1))
    return x2.reshape(batch, seq, D_MODEL)
```

```python
import functools

import jax
import jax.numpy as jnp
from jax import lax
from jax.experimental import pallas as pl
from jax.experimental.pallas import tpu as pltpu

D_MODEL = 2048
CONV_CH = 1024
CONV_K = 31
N_HEADS = 8
QK_NOPE = 128
QK_ROPE = 64
V_HEAD = 128
QK_HEAD = QK_NOPE + QK_ROPE
Q_LORA = 768
KV_LORA = 512
ATTN_CH = N_HEADS * V_HEAD
D_FF = 5632
ROPE_THETA = 10000.0
EPS = 1e-6

LANES = 128
HALF_ROPE = QK_ROPE // 2
HEAD_PAD = QK_NOPE + LANES
N_SLABS = CONV_CH // LANES
CONV_HALO = 32
V7X_VMEM_BYTES = 64 * 1024 * 1024
NEG_BIG = -1e30
LOG2_E = 1.4426950408889634

C_Q = 2 * CONV_CH
C_KV = C_Q + Q_LORA
C_KR = C_KV + KV_LORA

TM_PROJ = 512
TQ_ATTN = 512
TM_FFN = 512
TF_FFN = 512

BF16 = jnp.bfloat16
F32 = jnp.float32


def _vmem_limit(*buffer_bytes):
    need = 2 * sum(buffer_bytes)
    return int(min(max(need, 16 * 1024 * 1024), V7X_VMEM_BYTES - 4 * 1024 * 1024))


def _nbytes(shape, dtype):
    n = 1
    for s in shape:
        n *= s
    return n * jnp.dtype(dtype).itemsize


def _dot(a, b):
    return jnp.dot(a, b, preferred_element_type=F32)


def _rms(x, g):
    return x * lax.rsqrt(jnp.mean(x * x, axis=-1, keepdims=True) + EPS) * g


def _sigmoid(x):
    return 1.0 / (1.0 + jnp.exp(-x))


def _resident(shape):
    return pl.BlockSpec(shape, lambda *_: (0,) * len(shape), pipeline_mode=pl.Buffered(1))


def _conv_group(win, y_scr, w_ref, cb_ref, lng_ref, lnb_ref, go_ref, o_ref, tm):
    rows = 128
    first_tap = CONV_HALO - (CONV_K - 1)
    for c in range(N_SLABS):
        for r0 in range(0, tm, rows):
            acc = jnp.broadcast_to(cb_ref[c], (rows, LANES))
            for k in range(CONV_K):
                acc = acc + w_ref[c, k:k + 1, :] * win[c, r0 + first_tap + k:r0 + first_tap + k + rows, :]
            y_scr[c, r0:r0 + rows, :] = acc

    rows_n = 32
    inv_ch = 1.0 / CONV_CH
    total = lambda parts: jnp.sum(functools.reduce(lambda p, q: p + q, parts), axis=-1, keepdims=True)
    for r0 in range(0, tm, rows_n):
        ys = [y_scr[c, r0:r0 + rows_n, :] for c in range(N_SLABS)]
        mu = total(ys) * inv_ch
        ds = [y - mu for y in ys]
        rs = lax.rsqrt(total([d * d for d in ds]) * inv_ch + EPS)
        zs = []
        for c in range(N_SLABS):
            z = ds[c] * rs * lng_ref[c] + lnb_ref[c]
            zs.append(z * _sigmoid(z))
        rs2 = lax.rsqrt(total([z * z for z in zs]) * inv_ch + EPS)
        for c in range(N_SLABS):
            o_ref[r0:r0 + rows_n, c * LANES:(c + 1) * LANES] = (zs[c] * rs2 * go_ref[c]).astype(BF16)


def _in_proj_kernel(tiles_per_seq, x_ref, pos_ref, freq_ref, g_pre_ref, w_in_ref, w_kr_ref, g_q_ref,
                    w_uqt_ref, g_kv_ref, w_uk_ref, w_uvt_ref, cw_ref, cb_ref, lng_ref, lnb_ref,
                    go_ref, uc_ref, qt_ref, k_ref, vt_ref, win, y_scr):
    tm = x_ref.shape[0]
    starts_sequence = pl.program_id(0) % tiles_per_seq == 0

    @pl.when(starts_sequence)
    def _():
        win[:, 0:CONV_HALO, :] = jnp.zeros((N_SLABS, CONV_HALO, LANES), F32)

    @pl.when(jnp.logical_not(starts_sequence))
    def _():
        win[:, 0:CONV_HALO, :] = win[:, tm:tm + CONV_HALO, :]

    h = _rms(x_ref[...], g_pre_ref[...]).astype(BF16)

    a = _dot(h, w_in_ref[:, 0:CONV_CH])
    g = _dot(h, w_in_ref[:, CONV_CH:C_Q])
    u0 = a * _sigmoid(g)
    for c in range(N_SLABS):
        win[c, CONV_HALO:CONV_HALO + tm, :] = u0[:, c * LANES:(c + 1) * LANES]
    _conv_group(win, y_scr, cw_ref, cb_ref, lng_ref, lnb_ref, go_ref, uc_ref, tm)

    ang_t = freq_ref[...] * pos_ref[...].astype(F32)
    row = lax.broadcasted_iota(jnp.int32, ang_t.shape, 0)
    table_t = jnp.where(row < QK_ROPE, jnp.cos(ang_t), jnp.sin(ang_t))
    table = table_t.T

    scale = QK_HEAD ** -0.5 * LOG2_E
    qn = _rms(_dot(h, w_in_ref[:, C_Q:C_KV]), g_q_ref[...])
    q_all_t = _dot(w_uqt_ref[...], qn.T.astype(BF16)) * scale
    for hh in range(N_HEADS):
        o = hh * HEAD_PAD
        r = q_all_t[o + QK_NOPE:o + HEAD_PAD, :] * table_t
        qt_ref[hh, 0, 0:QK_NOPE, :] = q_all_t[o:o + QK_NOPE, :].astype(BF16)
        qt_ref[hh, 0, QK_NOPE:QK_HEAD, :] = (r[0:QK_ROPE, :] + r[QK_ROPE:, :]).astype(BF16)
        qt_ref[hh, 0, QK_HEAD:HEAD_PAD, :] = jnp.zeros((HEAD_PAD - QK_HEAD, r.shape[1]), BF16)

    kvn = _rms(_dot(h, w_in_ref[:, C_KV:C_KR]), g_kv_ref[...])
    k_all = _dot(kvn.astype(BF16), w_uk_ref[...])
    v_all_t = _dot(w_uvt_ref[...], kvn.T.astype(BF16))
    rk = _dot(h, w_kr_ref[...]) * table
    lane = lax.broadcasted_iota(jnp.int32, rk.shape, 1)
    k_pe = jnp.where(lane < QK_ROPE, rk + pltpu.roll(rk, QK_ROPE, axis=1), 0.0).astype(BF16)
    for hh in range(N_HEADS):
        k_ref[hh, :, 0:QK_NOPE] = k_all[:, hh * QK_NOPE:(hh + 1) * QK_NOPE].astype(BF16)
        k_ref[hh, :, QK_NOPE:HEAD_PAD] = k_pe
        vt_ref[hh, 0] = v_all_t[hh * V_HEAD:(hh + 1) * V_HEAD, :].astype(BF16)


def _in_proj(x2, pos_row, freq_col, g_pre, w_in_b, w_kr, g_q, w_uqt, g_kv, w_uk, w_uvt,
             w_slab, cb, lng, lnb, go, seq):
    t = x2.shape[0]
    tm = TM_PROJ
    nt = t // tm
    row = lambda i: (i, 0)
    slab_vec = (N_SLABS, 1, LANES)
    out_shapes = (
        jax.ShapeDtypeStruct((t, CONV_CH), BF16),
        jax.ShapeDtypeStruct((N_HEADS, nt, HEAD_PAD, tm), BF16),
        jax.ShapeDtypeStruct((N_HEADS, t, HEAD_PAD), BF16),
        jax.ShapeDtypeStruct((N_HEADS, nt, V_HEAD, tm), BF16),
    )
    limit = _vmem_limit(
        2 * _nbytes((tm, D_MODEL), F32), _nbytes(w_in_b.shape, BF16), _nbytes(w_kr.shape, BF16),
        _nbytes(w_uqt.shape, BF16), _nbytes(w_uk.shape, BF16), _nbytes(w_uvt.shape, BF16),
        2 * _nbytes((tm, CONV_CH), BF16), 4 * _nbytes((N_HEADS, tm, HEAD_PAD), BF16),
        2 * _nbytes((N_HEADS, tm, V_HEAD), BF16),
        2 * _nbytes((N_SLABS, tm + CONV_HALO, LANES), F32))
    return pl.pallas_call(
        functools.partial(_in_proj_kernel, seq // tm),
        grid=(nt,),
        in_specs=[
            pl.BlockSpec((tm, D_MODEL), row),
            pl.BlockSpec((1, tm), lambda i: (0, i)),
            _resident((LANES, 1)),
            _resident((1, D_MODEL)),
            _resident(w_in_b.shape),
            _resident(w_kr.shape),
            _resident((1, Q_LORA)),
            _resident(w_uqt.shape),
            _resident((1, KV_LORA)),
            _resident(w_uk.shape),
            _resident(w_uvt.shape),
            _resident(w_slab.shape),
            _resident(slab_vec), _resident(slab_vec), _resident(slab_vec), _resident(slab_vec),
        ],
        out_specs=(
            pl.BlockSpec((tm, CONV_CH), row),
            pl.BlockSpec((N_HEADS, 1, HEAD_PAD, tm), lambda i: (0, i, 0, 0)),
            pl.BlockSpec((N_HEADS, tm, HEAD_PAD), lambda i: (0, i, 0)),
            pl.BlockSpec((N_HEADS, 1, V_HEAD, tm), lambda i: (0, i, 0, 0)),
        ),
        out_shape=out_shapes,
        scratch_shapes=[
            pltpu.VMEM((N_SLABS, tm + CONV_HALO, LANES), F32),
            pltpu.VMEM((N_SLABS, tm, LANES), F32),
        ],
        compiler_params=pltpu.CompilerParams(
            dimension_semantics=("arbitrary",), vmem_limit_bytes=limit),
        name="in_proj",
    )(x2, pos_row, freq_col, g_pre, w_in_b, w_kr, g_q, w_uqt, g_kv, w_uk, w_uvt,
      w_slab, cb, lng, lnb, go)


def _attn_kernel(qt_ref, k_ref, vt_ref, g_ref, o_ref, m_scr, l_scr, acc_scr):
    tq = qt_ref.shape[3]
    qi = pl.program_id(1)
    key = lax.broadcasted_iota(jnp.int32, (tq, tq), 0)
    qry = lax.broadcasted_iota(jnp.int32, (tq, tq), 1)
    causal = key <= qry

    m_scr[...] = jnp.full(m_scr.shape, NEG_BIG, F32)
    l_scr[...] = jnp.zeros(l_scr.shape, F32)
    acc_scr[...] = jnp.zeros(acc_scr.shape, F32)

    def key_block(kb, diagonal):
        rows = pl.ds(pl.multiple_of(kb * tq, tq), tq)
        scores = lambda hh: _dot(k_ref[hh, rows, :], qt_ref[hh, 0])
        s_next = scores(0)
        for hh in range(N_HEADS):
            s = s_next
            if hh + 1 < N_HEADS:
                s_next = scores(hh + 1)
            if diagonal:
                s = jnp.where(causal, s, NEG_BIG)
            m = m_scr[hh]
            m_new = jnp.maximum(m, jnp.max(s, axis=0, keepdims=True))
            alpha = jnp.exp2(m - m_new)
            p = jnp.exp2(s - m_new)
            l_scr[hh] = alpha * l_scr[hh] + jnp.sum(p, axis=0, keepdims=True)
            acc_scr[hh] = alpha * acc_scr[hh] + _dot(vt_ref[hh, kb], p.astype(BF16))
            m_scr[hh] = m_new

    def full_block(kb, carry):
        key_block(kb, False)
        return carry

    lax.fori_loop(0, qi, full_block, 0)
    key_block(qi, True)

    ms = jnp.zeros((1, tq), F32)
    for hh in range(N_HEADS):
        o_t = acc_scr[hh] / l_scr[hh]
        acc_scr[hh] = o_t
        ms = ms + jnp.sum(o_t * o_t, axis=0, keepdims=True)
    rs = lax.rsqrt(ms * (1.0 / ATTN_CH) + EPS)
    for hh in range(N_HEADS):
        sl = slice(hh * V_HEAD, (hh + 1) * V_HEAD)
        o_ref[:, sl] = (acc_scr[hh] * rs * g_ref[sl, :]).T.astype(BF16)


def _attention(q_t, k, v_t, g, batch, seq):
    tq = TQ_ATTN
    nq = seq // tq
    limit = _vmem_limit(
        2 * _nbytes((N_HEADS, tq, HEAD_PAD), BF16), 2 * _nbytes((N_HEADS, seq, HEAD_PAD), BF16),
        2 * _nbytes((N_HEADS, seq, V_HEAD), BF16), 2 * _nbytes((tq, ATTN_CH), BF16),
        _nbytes((N_HEADS, tq, V_HEAD), F32), 4 * _nbytes((tq, tq), F32))
    return pl.pallas_call(
        _attn_kernel,
        grid=(batch, nq),
        in_specs=[
            pl.BlockSpec((N_HEADS, 1, HEAD_PAD, tq), lambda b, i: (0, b * nq + i, 0, 0)),
            pl.BlockSpec((N_HEADS, seq, HEAD_PAD), lambda b, i: (0, b, 0)),
            pl.BlockSpec((N_HEADS, nq, V_HEAD, tq), lambda b, i: (0, b, 0, 0)),
            _resident((ATTN_CH, 1)),
        ],
        out_specs=pl.BlockSpec((tq, ATTN_CH), lambda b, i: (b * nq + i, 0)),
        out_shape=jax.ShapeDtypeStruct((batch * seq, ATTN_CH), BF16),
        scratch_shapes=[
            pltpu.VMEM((N_HEADS, 1, tq), F32),
            pltpu.VMEM((N_HEADS, 1, tq), F32),
            pltpu.VMEM((N_HEADS, V_HEAD, tq), F32),
        ],
        compiler_params=pltpu.CompilerParams(
            dimension_semantics=("arbitrary", "arbitrary"), vmem_limit_bytes=limit),
        name="attention",
    )(q_t, k, v_t, g)


def _out_proj_kernel(uc_ref, at_ref, x_ref, w_ref, g_pm_ref, g_pf_ref, x1_ref, hf_ref):
    mix = _dot(uc_ref[...], w_ref[0:CONV_CH, :]) + _dot(at_ref[...], w_ref[CONV_CH:, :])
    x1 = x_ref[...] + _rms(mix, g_pm_ref[...])
    x1_ref[...] = x1
    hf_ref[...] = _rms(x1, g_pf_ref[...]).astype(BF16)


def _out_proj(uc, at, x2, w_out_b, g_pm, g_pf):
    t = x2.shape[0]
    tm = TM_PROJ
    row = lambda i: (i, 0)
    limit = _vmem_limit(
        4 * _nbytes((tm, CONV_CH), BF16), 4 * _nbytes((tm, D_MODEL), F32),
        _nbytes(w_out_b.shape, BF16), 2 * _nbytes((tm, D_MODEL), BF16))
    return pl.pallas_call(
        _out_proj_kernel,
        grid=(t // tm,),
        in_specs=[
            pl.BlockSpec((tm, CONV_CH), row),
            pl.BlockSpec((tm, ATTN_CH), row),
            pl.BlockSpec((tm, D_MODEL), row),
            _resident(w_out_b.shape),
            _resident((1, D_MODEL)),
            _resident((1, D_MODEL)),
        ],
        out_specs=(pl.BlockSpec((tm, D_MODEL), row), pl.BlockSpec((tm, D_MODEL), row)),
        out_shape=(jax.ShapeDtypeStruct((t, D_MODEL), F32), jax.ShapeDtypeStruct((t, D_MODEL), BF16)),
        compiler_params=pltpu.CompilerParams(
            dimension_semantics=("arbitrary",), vmem_limit_bytes=limit),
        name="out_proj",
    )(uc, at, x2, w_out_b, g_pm, g_pf)


def _ffn_kernel(hf_ref, x1_ref, wg_ref, wu_ref, wd_ref, g_ref, o_ref):
    j = pl.program_id(1)

    @pl.when(j == 0)
    def _():
        o_ref[...] = jnp.zeros_like(o_ref)

    hf = hf_ref[...]
    gate = _dot(hf, wg_ref[...])
    up = _dot(hf, wu_ref[...])
    act = (gate * _sigmoid(gate) * up).astype(BF16)
    o_ref[...] += _dot(act, wd_ref[...])

    @pl.when(j == pl.num_programs(1) - 1)
    def _():
        o_ref[...] = x1_ref[...] + _rms(o_ref[...], g_ref[...])


def _ffn(hf, x1, wg_b, wu_b, wd_b, g):
    t = hf.shape[0]
    tm, tf = TM_FFN, TF_FFN
    limit = _vmem_limit(
        2 * _nbytes((tm, D_MODEL), BF16), 4 * _nbytes((tm, D_MODEL), F32),
        6 * _nbytes((D_MODEL, tf), BF16), 3 * _nbytes((tm, tf), F32))
    return pl.pallas_call(
        _ffn_kernel,
        grid=(t // tm, D_FF // tf),
        in_specs=[
            pl.BlockSpec((tm, D_MODEL), lambda i, j: (i, 0)),
            pl.BlockSpec((tm, D_MODEL), lambda i, j: (i, 0)),
            pl.BlockSpec((D_MODEL, tf), lambda i, j: (0, j)),
            pl.BlockSpec((D_MODEL, tf), lambda i, j: (0, j)),
            pl.BlockSpec((tf, D_MODEL), lambda i, j: (j, 0)),
            _resident((1, D_MODEL)),
        ],
        out_specs=pl.BlockSpec((tm, D_MODEL), lambda i, j: (i, 0)),
        out_shape=jax.ShapeDtypeStruct((t, D_MODEL), F32),
        compiler_params=pltpu.CompilerParams(
            dimension_semantics=("arbitrary", "arbitrary"), vmem_limit_bytes=limit),
        name="ffn",
    )(hf, x1, wg_b, wu_b, wd_b, g)


def _rope_columns(w):
    w1, w2 = w[..., :HALF_ROPE], w[..., HALF_ROPE:]
    return jnp.concatenate([w1, w2, -w2, w1], axis=-1)


def _slab_vec(v):
    return v.reshape(N_SLABS, 1, LANES)


def kernel(x, positions, pre_mix_norm, w_in, q_norm, w_uq, kv_norm, w_ukv, conv_w, conv_b,
           conv_ln_g, conv_ln_b, conv_out_norm, attn_out_norm, w_out, post_mix_norm,
           pre_ffn_norm, w_gate, w_up, w_down, post_ffn_norm):
    batch, seq, _ = x.shape
    depth = w_in.shape[0]
    tokens = batch * seq
    x2 = x.reshape(tokens, D_MODEL)
    pos_row = positions.reshape(1, tokens)
    inv_freq = ROPE_THETA ** (-jnp.arange(0, QK_ROPE, 2, dtype=F32) / QK_ROPE)
    freq_col = jnp.tile(inv_freq, LANES // HALF_ROPE).reshape(LANES, 1)

    for l in range(depth):
        w_in_b = w_in[l][:, :C_KR].astype(BF16)
        w_kr = _rope_columns(w_in[l][:, C_KR:]).astype(BF16)
        wq = w_uq[l].reshape(Q_LORA, N_HEADS, QK_HEAD)
        w_uqt = jnp.concatenate([wq[..., :QK_NOPE], _rope_columns(wq[..., QK_NOPE:])], axis=-1).reshape(
            Q_LORA, N_HEADS * HEAD_PAD).T.astype(BF16)
        wkv = w_ukv[l].reshape(KV_LORA, N_HEADS, QK_NOPE + V_HEAD)
        w_uk = wkv[..., :QK_NOPE].reshape(KV_LORA, N_HEADS * QK_NOPE).astype(BF16)
        w_uvt = wkv[..., QK_NOPE:].reshape(KV_LORA, N_HEADS * V_HEAD).T.astype(BF16)
        w_out_b = w_out[l].astype(BF16)
        wg_b, wu_b, wd_b = w_gate[l].astype(BF16), w_up[l].astype(BF16), w_down[l].astype(BF16)
        conv_w_pad = jnp.pad(conv_w[l], ((0, CONV_HALO - CONV_K), (0, 0)))
        w_slab = conv_w_pad.reshape(CONV_HALO, N_SLABS, LANES).transpose(1, 0, 2)

        uc, q_t, k, v_t = _in_proj(
            x2, pos_row, freq_col, pre_mix_norm[l].reshape(1, -1), w_in_b, w_kr,
            q_norm[l].reshape(1, -1), w_uqt, kv_norm[l].reshape(1, -1), w_uk, w_uvt,
            w_slab, _slab_vec(conv_b[l]), _slab_vec(conv_ln_g[l]), _slab_vec(conv_ln_b[l]),
            _slab_vec(conv_out_norm[l]), seq)
        at = _attention(q_t, k, v_t, attn_out_norm[l].reshape(-1, 1), batch, seq)
        x1, hf = _out_proj(uc, at, x2, w_out_b, post_mix_norm[l].reshape(1, -1),
                           pre_ffn_norm[l].reshape(1, -1))
        x2 = _ffn(hf, x1, wg_b, wu_b, wd_b, post_ffn_norm[l].reshape(1, -1))
    return x2.reshape(batch, seq, D_MODEL)
```

```python
import functools

import jax
import jax.numpy as jnp
from jax import lax
from jax.experimental import pallas as pl
from jax.experimental.pallas import tpu as pltpu

D_MODEL = 2048
CONV_CH = 1024
CONV_K = 31
N_HEADS = 8
QK_NOPE = 128
QK_ROPE = 64
V_HEAD = 128
QK_HEAD = QK_NOPE + QK_ROPE
Q_LORA = 768
KV_LORA = 512
ATTN_CH = N_HEADS * V_HEAD
D_FF = 5632
ROPE_THETA = 10000.0
EPS = 1e-6

LANES = 128
HALF_ROPE = QK_ROPE // 2
HEAD_PAD = QK_NOPE + LANES
N_SLABS = CONV_CH // LANES
CONV_HALO = 32
V7X_VMEM_BYTES = 64 * 1024 * 1024
NEG_BIG = -1e30
LOG2_E = 1.4426950408889634

C_Q = 2 * CONV_CH
C_KV = C_Q + Q_LORA
C_KR = C_KV + KV_LORA

W_PREP_ROWS = 256
TM_PROJ = 512
OUT_ROW_GROUPS = 2
TQ_ATTN = 512
TM_FFN = 1024
TF_FFN = 512
FFN_X1_PIECES = 8
FFN_ACT_COLS = 256
FFN_DOWN_COLS = 512
FFN_LAST_GROUPS = 4

BF16 = jnp.bfloat16
F32 = jnp.float32


def _vmem_limit(*buffer_bytes):
    need = 2 * sum(buffer_bytes)
    return int(min(max(need, 16 * 1024 * 1024), V7X_VMEM_BYTES - 4 * 1024 * 1024))


def _nbytes(shape, dtype):
    n = 1
    for s in shape:
        n *= s
    return n * jnp.dtype(dtype).itemsize


def _dot(a, b):
    return jnp.dot(a, b, preferred_element_type=F32)


def _rms(x, g):
    return x * lax.rsqrt(jnp.mean(x * x, axis=-1, keepdims=True) + EPS) * g


def _sigmoid(x):
    return 1.0 / (1.0 + jnp.exp(-x))


def _resident(shape):
    return pl.BlockSpec(shape, lambda *_: (0,) * len(shape), pipeline_mode=pl.Buffered(1))


def _w_prep_kernel(wt_ref, wb_ref, wkr_ref):
    j = pl.program_id(0)
    n_main = C_KR // W_PREP_ROWS

    @pl.when(j < n_main)
    def _():
        wb_ref[...] = wt_ref[...].T.astype(BF16)

    @pl.when(j == n_main)
    def _():
        x1 = wt_ref[0:HALF_ROPE, :]
        x2 = wt_ref[HALF_ROPE:QK_ROPE, :]
        wkr_ref[...] = jnp.concatenate([x1, x2, -x2, x1], axis=0).T.astype(BF16)


def _w_prep(w_in_t):
    n_main = C_KR // W_PREP_ROWS
    return pl.pallas_call(
        _w_prep_kernel,
        grid=(n_main + 1,),
        in_specs=[pl.BlockSpec((W_PREP_ROWS, D_MODEL), lambda j: (j, 0))],
        out_specs=(pl.BlockSpec((D_MODEL, W_PREP_ROWS), lambda j: (0, jnp.minimum(j, n_main - 1))),
                   pl.BlockSpec((D_MODEL, LANES), lambda j: (0, 0))),
        out_shape=(jax.ShapeDtypeStruct((D_MODEL, C_KR), BF16),
                   jax.ShapeDtypeStruct((D_MODEL, LANES), BF16)),
        compiler_params=pltpu.CompilerParams(
            dimension_semantics=("arbitrary",),
            vmem_limit_bytes=_vmem_limit(6 * _nbytes((W_PREP_ROWS, D_MODEL), F32))),
        name="w_prep",
    )(w_in_t)


def _conv_group(win, y_scr, w_ref, cb_ref, lng_ref, lnb_ref, go_ref, o_ref, tm):
    rows = 128
    first_tap = CONV_HALO - (CONV_K - 1)
    for c in range(N_SLABS):
        for r0 in range(0, tm, rows):
            acc = jnp.broadcast_to(cb_ref[c], (rows, LANES))
            for k in range(CONV_K):
                acc = acc + w_ref[c, k:k + 1, :] * win[c, r0 + first_tap + k:r0 + first_tap + k + rows, :]
            y_scr[c, r0:r0 + rows, :] = acc

    rows_n = 32
    inv_ch = 1.0 / CONV_CH
    total = lambda parts: jnp.sum(functools.reduce(lambda p, q: p + q, parts), axis=-1, keepdims=True)
    for r0 in range(0, tm, rows_n):
        ys = [y_scr[c, r0:r0 + rows_n, :] for c in range(N_SLABS)]
        mu = total(ys) * inv_ch
        ds = [y - mu for y in ys]
        rs = lax.rsqrt(total([d * d for d in ds]) * inv_ch + EPS)
        zs = []
        for c in range(N_SLABS):
            z = ds[c] * rs * lng_ref[c] + lnb_ref[c]
            zs.append(z * _sigmoid(z))
        rs2 = lax.rsqrt(total([z * z for z in zs]) * inv_ch + EPS)
        for c in range(N_SLABS):
            o_ref[r0:r0 + rows_n, c * LANES:(c + 1) * LANES] = (zs[c] * rs2 * go_ref[c]).astype(BF16)


def _in_proj_kernel(tiles_per_seq, x_ref, pos_ref, freq_ref, g_pre_ref, w_in_ref, w_kr_ref, g_q_ref,
                    w_uqt_ref, g_kv_ref, w_uk_ref, w_uvt_ref, cw_ref, cb_ref, lng_ref, lnb_ref,
                    go_ref, wff_ref, uc_ref, qt_ref, k_ref, vt_ref, wff_b_ref, win, y_scr):
    wff_b_ref[...] = wff_ref[...].astype(BF16)

    tm = x_ref.shape[0]
    starts_sequence = pl.program_id(0) % tiles_per_seq == 0

    @pl.when(starts_sequence)
    def _():
        win[:, 0:CONV_HALO, :] = jnp.zeros((N_SLABS, CONV_HALO, LANES), F32)

    @pl.when(jnp.logical_not(starts_sequence))
    def _():
        win[:, 0:CONV_HALO, :] = win[:, tm:tm + CONV_HALO, :]

    h = _rms(x_ref[...], g_pre_ref[...]).astype(BF16)

    ag = _dot(h, w_in_ref[:, 0:C_Q])
    u0 = ag[:, 0:CONV_CH] * _sigmoid(ag[:, CONV_CH:])
    lat = _dot(h, w_in_ref[:, C_Q:C_KR])
    for c in range(N_SLABS):
        win[c, CONV_HALO:CONV_HALO + tm, :] = u0[:, c * LANES:(c + 1) * LANES]
    _conv_group(win, y_scr, cw_ref, cb_ref, lng_ref, lnb_ref, go_ref, uc_ref, tm)

    ang_t = freq_ref[...] * pos_ref[...].astype(F32)
    cos_t, sin_t = jnp.cos(ang_t), jnp.sin(ang_t)
    table_t = jnp.concatenate([cos_t, cos_t, sin_t, sin_t], axis=0)
    table = table_t.T

    scale = QK_HEAD ** -0.5 * LOG2_E
    qn = _rms(lat[:, 0:Q_LORA], g_q_ref[...])
    q_all_t = _dot(w_uqt_ref[...], qn.T.astype(BF16)) * scale
    for hh in range(N_HEADS):
        o = hh * HEAD_PAD
        r = q_all_t[o + QK_NOPE:o + HEAD_PAD, :] * table_t
        qt_ref[hh, 0, 0:QK_NOPE, :] = q_all_t[o:o + QK_NOPE, :].astype(BF16)
        qt_ref[hh, 0, QK_NOPE:QK_HEAD, :] = (r[0:QK_ROPE, :] + r[QK_ROPE:, :]).astype(BF16)
        qt_ref[hh, 0, QK_HEAD:HEAD_PAD, :] = jnp.zeros((HEAD_PAD - QK_HEAD, tm), BF16)

    kvn = _rms(lat[:, Q_LORA:], g_kv_ref[...])
    k_all = _dot(kvn.astype(BF16), w_uk_ref[...])
    v_all_t = _dot(w_uvt_ref[...], kvn.T.astype(BF16))
    rk = _dot(h, w_kr_ref[...]) * table
    lane = lax.broadcasted_iota(jnp.int32, rk.shape, 1)
    k_pe = jnp.where(lane < QK_ROPE, rk + pltpu.roll(rk, QK_ROPE, axis=1), 0.0).astype(BF16)
    for hh in range(N_HEADS):
        k_ref[hh, :, 0:QK_NOPE] = k_all[:, hh * QK_NOPE:(hh + 1) * QK_NOPE].astype(BF16)
        k_ref[hh, :, QK_NOPE:HEAD_PAD] = k_pe
        vt_ref[hh, 0] = v_all_t[hh * V_HEAD:(hh + 1) * V_HEAD, :].astype(BF16)


def _in_proj(x2, pos_row, freq_col, g_pre, w_in_b, w_kr, g_q, w_uqt, g_kv, w_uk, w_uvt,
             w_slab, cb, lng, lnb, go, w_ff, seq):
    t = x2.shape[0]
    tm = TM_PROJ
    nt = t // tm
    ff_block = (w_ff.shape[0] // nt, w_ff.shape[1])
    row = lambda i: (i, 0)
    slab_vec = (N_SLABS, 1, LANES)
    out_shapes = (
        jax.ShapeDtypeStruct((t, CONV_CH), BF16),
        jax.ShapeDtypeStruct((N_HEADS, nt, HEAD_PAD, tm), BF16),
        jax.ShapeDtypeStruct((N_HEADS, t, HEAD_PAD), BF16),
        jax.ShapeDtypeStruct((N_HEADS, nt, V_HEAD, tm), BF16),
        jax.ShapeDtypeStruct(w_ff.shape, BF16),
    )
    limit = _vmem_limit(
        3 * _nbytes(ff_block, F32),
        2 * _nbytes((tm, D_MODEL), F32), _nbytes(w_in_b.shape, BF16), _nbytes(w_kr.shape, BF16),
        _nbytes(w_uqt.shape, BF16), _nbytes(w_uk.shape, BF16), _nbytes(w_uvt.shape, BF16),
        2 * _nbytes((tm, CONV_CH), BF16), 4 * _nbytes((N_HEADS, tm, HEAD_PAD), BF16),
        2 * _nbytes((N_HEADS, tm, V_HEAD), BF16),
        2 * _nbytes((N_SLABS, tm + CONV_HALO, LANES), F32))
    return pl.pallas_call(
        functools.partial(_in_proj_kernel, seq // tm),
        grid=(nt,),
        in_specs=[
            pl.BlockSpec((tm, D_MODEL), row),
            pl.BlockSpec((1, tm), lambda i: (0, i)),
            _resident((HALF_ROPE, 1)),
            _resident((1, D_MODEL)),
            _resident(w_in_b.shape),
            _resident(w_kr.shape),
            _resident((1, Q_LORA)),
            _resident(w_uqt.shape),
            _resident((1, KV_LORA)),
            _resident(w_uk.shape),
            _resident(w_uvt.shape),
            _resident(w_slab.shape),
            _resident(slab_vec), _resident(slab_vec), _resident(slab_vec), _resident(slab_vec),
            pl.BlockSpec(ff_block, row),
        ],
        out_specs=(
            pl.BlockSpec((tm, CONV_CH), row),
            pl.BlockSpec((N_HEADS, 1, HEAD_PAD, tm), lambda i: (0, i, 0, 0)),
            pl.BlockSpec((N_HEADS, tm, HEAD_PAD), lambda i: (0, i, 0)),
            pl.BlockSpec((N_HEADS, 1, V_HEAD, tm), lambda i: (0, i, 0, 0)),
            pl.BlockSpec(ff_block, row),
        ),
        out_shape=out_shapes,
        scratch_shapes=[
            pltpu.VMEM((N_SLABS, tm + CONV_HALO, LANES), F32),
            pltpu.VMEM((N_SLABS, tm, LANES), F32),
        ],
        compiler_params=pltpu.CompilerParams(
            dimension_semantics=("arbitrary",), vmem_limit_bytes=limit),
        name="in_proj",
    )(x2, pos_row, freq_col, g_pre, w_in_b, w_kr, g_q, w_uqt, g_kv, w_uk, w_uvt,
      w_slab, cb, lng, lnb, go, w_ff)


def _attn_kernel(qt_ref, k_ref, vt_ref, g_ref, wff_ref, wo_ref, o_ref, wff_b_ref, wo_b_ref,
                 m_scr, l_scr, acc_scr):
    wff_b_ref[...] = wff_ref[...].astype(BF16)
    wo_b_ref[...] = wo_ref[...].astype(BF16)

    tq = qt_ref.shape[3]
    qi = pl.program_id(1)

    m_scr[...] = jnp.full(m_scr.shape, NEG_BIG, F32)
    l_scr[...] = jnp.zeros(l_scr.shape, F32)
    acc_scr[...] = jnp.zeros(acc_scr.shape, F32)

    key = lax.broadcasted_iota(jnp.int32, (tq, tq), 0)
    qry = lax.broadcasted_iota(jnp.int32, (tq, tq), 1)
    causal = key <= qry

    def key_block(kb, diagonal):
        rows = pl.ds(pl.multiple_of(kb * tq, tq), tq)
        scores = lambda hh: _dot(k_ref[hh, rows, :], qt_ref[hh, 0])
        s_next = scores(0)
        for hh in range(N_HEADS):
            s = s_next
            if hh + 1 < N_HEADS:
                s_next = scores(hh + 1)
            if diagonal:
                s = jnp.where(causal, s, NEG_BIG)
            m = m_scr[hh]
            m_new = jnp.maximum(m, jnp.max(s, axis=0, keepdims=True))
            alpha = jnp.exp2(m - m_new)
            p = jnp.exp2(s - m_new)
            l_scr[hh] = alpha * l_scr[hh] + jnp.sum(p, axis=0, keepdims=True)
            acc_scr[hh] = alpha * acc_scr[hh] + _dot(vt_ref[hh, kb], p.astype(BF16))
            m_scr[hh] = m_new

    def full_block(kb, carry):
        key_block(kb, False)
        return carry

    lax.fori_loop(0, qi, full_block, 0)
    key_block(qi, True)

    ms = jnp.zeros((1, tq), F32)
    for hh in range(N_HEADS):
        o_t = acc_scr[hh] / l_scr[hh]
        acc_scr[hh] = o_t
        ms = ms + jnp.sum(o_t * o_t, axis=0, keepdims=True)
    rs = lax.rsqrt(ms * (1.0 / ATTN_CH) + EPS)
    for hh in range(N_HEADS):
        sl = slice(hh * V_HEAD, (hh + 1) * V_HEAD)
        o_ref[:, sl] = (acc_scr[hh] * rs * g_ref[sl, :]).T.astype(BF16)


def _attention(q_t, k, v_t, g, w_ff, w_out, batch, seq):
    tq = TQ_ATTN
    nq = seq // tq
    ff_block = (w_ff.shape[0] // (batch * nq), w_ff.shape[1])
    ff_spec = pl.BlockSpec(ff_block, lambda b, i: (b * nq + i, 0))
    wo_block = (w_out.shape[0] // (batch * nq), w_out.shape[1])
    wo_spec = pl.BlockSpec(wo_block, lambda b, i: (b * nq + i, 0))
    limit = _vmem_limit(
        3 * _nbytes(ff_block, F32), 3 * _nbytes(wo_block, F32),
        2 * _nbytes((N_HEADS, tq, HEAD_PAD), BF16), 2 * _nbytes((N_HEADS, seq, HEAD_PAD), BF16),
        2 * _nbytes((N_HEADS, seq, V_HEAD), BF16), 2 * _nbytes((tq, ATTN_CH), BF16),
        _nbytes((N_HEADS, tq, V_HEAD), F32), 4 * _nbytes((tq, tq), F32))
    return pl.pallas_call(
        _attn_kernel,
        grid=(batch, nq),
        in_specs=[
            pl.BlockSpec((N_HEADS, 1, HEAD_PAD, tq), lambda b, i: (0, b * nq + i, 0, 0)),
            pl.BlockSpec((N_HEADS, seq, HEAD_PAD), lambda b, i: (0, b, 0)),
            pl.BlockSpec((N_HEADS, nq, V_HEAD, tq), lambda b, i: (0, b, 0, 0)),
            _resident((ATTN_CH, 1)),
            ff_spec,
            wo_spec,
        ],
        out_specs=(pl.BlockSpec((tq, ATTN_CH), lambda b, i: (b * nq + i, 0)), ff_spec, wo_spec),
        out_shape=(jax.ShapeDtypeStruct((batch * seq, ATTN_CH), BF16),
                   jax.ShapeDtypeStruct(w_ff.shape, BF16),
                   jax.ShapeDtypeStruct(w_out.shape, BF16)),
        scratch_shapes=[
            pltpu.VMEM((N_HEADS, 1, tq), F32),
            pltpu.VMEM((N_HEADS, 1, tq), F32),
            pltpu.VMEM((N_HEADS, V_HEAD, tq), F32),
        ],
        compiler_params=pltpu.CompilerParams(
            dimension_semantics=("arbitrary", "arbitrary"), vmem_limit_bytes=limit),
        name="attention",
    )(q_t, k, v_t, g, w_ff, w_out)


def _out_proj_kernel(uc_ref, at_ref, x_ref, w_ref, g_pm_ref, g_pf_ref, wff_ref, x1_ref, hf_ref,
                     wff_b_ref):
    wff_b_ref[...] = wff_ref[...].astype(BF16)

    tm = x_ref.shape[0]
    rows = tm // OUT_ROW_GROUPS
    for r in range(0, tm, rows):
        sl = slice(r, r + rows)
        mix = _dot(uc_ref[sl, :], w_ref[0:CONV_CH, :]) + _dot(at_ref[sl, :], w_ref[CONV_CH:, :])
        x1 = x_ref[sl, :] + _rms(mix, g_pm_ref[...])
        x1_ref[sl, :] = x1
        hf_ref[sl, :] = _rms(x1, g_pf_ref[...]).astype(BF16)


def _out_proj(uc, at, x2, w_out_b, g_pm, g_pf, w_ff):
    t = x2.shape[0]
    tm = TM_PROJ
    row = lambda i: (i, 0)
    ff_block = (w_ff.shape[0] // (t // tm), w_ff.shape[1])
    limit = _vmem_limit(
        3 * _nbytes(ff_block, F32),
        4 * _nbytes((tm, CONV_CH), BF16), 4 * _nbytes((tm, D_MODEL), F32),
        _nbytes(w_out_b.shape, BF16), 2 * _nbytes((tm, D_MODEL), BF16))
    return pl.pallas_call(
        _out_proj_kernel,
        grid=(t // tm,),
        in_specs=[
            pl.BlockSpec((tm, CONV_CH), row),
            pl.BlockSpec((tm, ATTN_CH), row),
            pl.BlockSpec((tm, D_MODEL), row),
            _resident(w_out_b.shape),
            _resident((1, D_MODEL)),
            _resident((1, D_MODEL)),
            pl.BlockSpec(ff_block, row),
        ],
        out_specs=(pl.BlockSpec((tm, D_MODEL), row), pl.BlockSpec((tm, D_MODEL), row),
                   pl.BlockSpec(ff_block, row)),
        out_shape=(jax.ShapeDtypeStruct((t, D_MODEL), F32), jax.ShapeDtypeStruct((t, D_MODEL), BF16),
                   jax.ShapeDtypeStruct(w_ff.shape, BF16)),
        compiler_params=pltpu.CompilerParams(
            dimension_semantics=("arbitrary",), vmem_limit_bytes=limit),
        name="out_proj",
    )(uc, at, x2, w_out_b, g_pm, g_pf, w_ff)


def _ffn_kernel(hf_ref, x1_ref, wg_ref, wu_ref, wd_ref, g_ref, o_ref, x1_scr):
    j = pl.program_id(1)
    piece = x1_ref.shape[0]

    @pl.when(j == 0)
    def _():
        o_ref[...] = jnp.zeros_like(o_ref)

    @pl.when(j < FFN_X1_PIECES)
    def _():
        x1_scr[pl.ds(pl.multiple_of(j * piece, piece), piece), :] = x1_ref[...]

    def hidden_step(rows):
        hf = hf_ref[rows, :]
        acts = []
        for c in range(0, wg_ref.shape[1], FFN_ACT_COLS):
            gate = _dot(hf, wg_ref[:, c:c + FFN_ACT_COLS])
            up = _dot(hf, wu_ref[:, c:c + FFN_ACT_COLS])
            acts.append((gate * _sigmoid(gate) * up).astype(BF16))
        act = jnp.concatenate(acts, axis=1)
        for c in range(0, D_MODEL, FFN_DOWN_COLS):
            o_ref[rows, c:c + FFN_DOWN_COLS] += _dot(act, wd_ref[:, c:c + FFN_DOWN_COLS])

    last = pl.num_programs(1) - 1

    @pl.when(j < last)
    def _():
        hidden_step(slice(None))

    @pl.when(j == last)
    def _():
        tm = o_ref.shape[0]
        rows_per_group = tm // FFN_LAST_GROUPS
        for r in range(0, tm, rows_per_group):
            rows = slice(r, r + rows_per_group)
            hidden_step(rows)
            o_ref[rows, :] = x1_scr[rows, :] + _rms(o_ref[rows, :], g_ref[...])


def _ffn(hf, x1, wg_b, wu_b, wd_b, g):
    t = hf.shape[0]
    tm, tf = TM_FFN, TF_FFN
    piece = tm // FFN_X1_PIECES
    limit = _vmem_limit(
        2 * _nbytes((tm, D_MODEL), BF16), 3 * _nbytes((tm, D_MODEL), F32),
        2 * _nbytes((piece, D_MODEL), F32), 6 * _nbytes((D_MODEL, tf), BF16),
        3 * _nbytes((tm, tf), F32))
    return pl.pallas_call(
        _ffn_kernel,
        grid=(t // tm, D_FF // tf),
        in_specs=[
            pl.BlockSpec((tm, D_MODEL), lambda i, j: (i, 0)),
            pl.BlockSpec((piece, D_MODEL),
                         lambda i, j: (i * FFN_X1_PIECES + jnp.minimum(j, FFN_X1_PIECES - 1), 0)),
            pl.BlockSpec((D_MODEL, tf), lambda i, j: (0, j)),
            pl.BlockSpec((D_MODEL, tf), lambda i, j: (0, j)),
            pl.BlockSpec((tf, D_MODEL), lambda i, j: (j, 0)),
            _resident((1, D_MODEL)),
        ],
        out_specs=pl.BlockSpec((tm, D_MODEL), lambda i, j: (i, 0)),
        out_shape=jax.ShapeDtypeStruct((t, D_MODEL), F32),
        scratch_shapes=[pltpu.VMEM((tm, D_MODEL), F32)],
        compiler_params=pltpu.CompilerParams(
            dimension_semantics=("arbitrary", "arbitrary"), vmem_limit_bytes=limit),
        name="ffn",
    )(hf, x1, wg_b, wu_b, wd_b, g)


def _rope_columns(w):
    w1, w2 = w[..., :HALF_ROPE], w[..., HALF_ROPE:]
    return jnp.concatenate([w1, w2, -w2, w1], axis=-1)


def _slab_vec(v):
    return v.reshape(N_SLABS, 1, LANES)


def kernel(x, positions, pre_mix_norm, w_in, q_norm, w_uq, kv_norm, w_ukv, conv_w, conv_b,
           conv_ln_g, conv_ln_b, conv_out_norm, attn_out_norm, w_out, post_mix_norm,
           pre_ffn_norm, w_gate, w_up, w_down, post_ffn_norm):
    batch, seq, _ = x.shape
    depth = w_in.shape[0]
    tokens = batch * seq
    x2 = x.reshape(tokens, D_MODEL)
    pos_row = positions.reshape(1, tokens)
    inv_freq = ROPE_THETA ** (-jnp.arange(0, QK_ROPE, 2, dtype=F32) / QK_ROPE)
    freq_col = inv_freq.reshape(HALF_ROPE, 1)

    for l in range(depth):
        w_in_b, w_kr = _w_prep(w_in[l].T)
        wq = w_uq[l].reshape(Q_LORA, N_HEADS, QK_HEAD)
        w_uqt = jnp.concatenate([wq[..., :QK_NOPE], _rope_columns(wq[..., QK_NOPE:])], axis=-1).reshape(
            Q_LORA, N_HEADS * HEAD_PAD).T.astype(BF16)
        wkv = w_ukv[l].reshape(KV_LORA, N_HEADS, QK_NOPE + V_HEAD)
        w_uk = wkv[..., :QK_NOPE].reshape(KV_LORA, N_HEADS * QK_NOPE).astype(BF16)
        w_uvt = wkv[..., QK_NOPE:].reshape(KV_LORA, N_HEADS * V_HEAD).T.astype(BF16)
        conv_w_pad = jnp.pad(conv_w[l], ((0, CONV_HALO - CONV_K), (0, 0)))
        w_slab = conv_w_pad.reshape(CONV_HALO, N_SLABS, LANES).transpose(1, 0, 2)

        uc, q_t, k, v_t, wg_b = _in_proj(
            x2, pos_row, freq_col, pre_mix_norm[l].reshape(1, -1), w_in_b, w_kr,
            q_norm[l].reshape(1, -1), w_uqt, kv_norm[l].reshape(1, -1), w_uk, w_uvt,
            w_slab, _slab_vec(conv_b[l]), _slab_vec(conv_ln_g[l]), _slab_vec(conv_ln_b[l]),
            _slab_vec(conv_out_norm[l]), w_gate[l], seq)
        at, wu_b, w_out_b = _attention(q_t, k, v_t, attn_out_norm[l].reshape(-1, 1), w_up[l],
                                       w_out[l], batch, seq)
        x1, hf, wd_b = _out_proj(uc, at, x2, w_out_b, post_mix_norm[l].reshape(1, -1),
                                 pre_ffn_norm[l].reshape(1, -1), w_down[l])
        x2 = _ffn(hf, x1, wg_b, wu_b, wd_b, post_ffn_norm[l].reshape(1, -1))
    return x2.reshape(batch, seq, D_MODEL)
```
